```python
import jax, jax.numpy as jnp
from jax import lax
import numpy as np

D_MODEL = 1024
BATCH = 2
SEQ = 8192
DEPTH = 2

CHUNK = 64
CONV_WIDTH = D_MODEL
CONV_GROUPS = 16
CONV_K = 3
LRU_WIDTH = D_MODEL
LRU_HEADS = 16
LRU_HEAD_DIM = LRU_WIDTH // LRU_HEADS
LRU_CONV_K = 4
LRU_C = 8.0
D_FF = ((-(-8 * D_MODEL // 3)) + 255) // 256 * 256
RMS_EPS = 1e-6
IN_WIDTHS = (CONV_WIDTH, CONV_WIDTH, CONV_WIDTH, LRU_WIDTH, LRU_WIDTH, D_MODEL, D_MODEL)
IN_TOTAL = sum(IN_WIDTHS)
SPLIT_POINTS = tuple(int(v) for v in np.cumsum(IN_WIDTHS)[:-1])

kernel_name = "hybrid_shortconv_rglru_gated_merge"


def rmsnorm(x, g):
    xf = x.astype(jnp.float32)
    var = jnp.mean(xf * xf, axis=-1, keepdims=True)
    return (xf * lax.rsqrt(var + RMS_EPS) * g.astype(jnp.float32)).astype(x.dtype)


def causal_depthwise_conv(x, w, b=None):
    K = w.shape[0]
    S = x.shape[1]
    xp = jnp.pad(x, ((0, 0), (K - 1, 0), (0, 0)))
    y = xp[:, 0:S] * w[0]
    for k in range(1, K):
        y = y + xp[:, k:k + S] * w[k]
    if b is not None:
        y = y + b
    return y


def rg_lru(x, w_a, b_a, w_x, b_x, lam):
    Bsz, S, W = x.shape
    f32 = jnp.float32
    xf = x.astype(f32)
    xh = xf.reshape(Bsz, S, LRU_HEADS, LRU_HEAD_DIM)
    r = jax.nn.sigmoid(jnp.einsum('bshd,hde->bshe', xh, w_a.astype(f32)).reshape(Bsz, S, W) + b_a.astype(f32))
    i = jax.nn.sigmoid(jnp.einsum('bshd,hde->bshe', xh, w_x.astype(f32)).reshape(Bsz, S, W) + b_x.astype(f32))
    log_a = -LRU_C * r * jax.nn.softplus(-lam.astype(f32))
    a = jnp.exp(log_a)
    b = jnp.sqrt(-jnp.expm1(2.0 * log_a)) * (i * xf)
    n_chunks = S // CHUNK
    a_c = a.reshape(Bsz, n_chunks, CHUNK, W)
    b_c = b.reshape(Bsz, n_chunks, CHUNK, W)

    def combine(left, right):
        a_l, b_l = left
        a_r, b_r = right
        return a_l * a_r, a_r * b_l + b_r

    a_cum, h_loc = lax.associative_scan(combine, (a_c, b_c), axis=2)

    def step(h_prev, inp):
        a_cum_k, h_loc_k = inp
        h = h_loc_k + a_cum_k * h_prev[:, None, :]
        return h[:, -1], h

    h0 = jnp.zeros((Bsz, W), f32)
    _, hs = lax.scan(step, h0, (jnp.moveaxis(a_cum, 1, 0), jnp.moveaxis(h_loc, 1, 0)))
    return jnp.moveaxis(hs, 0, 1).reshape(Bsz, S, W).astype(x.dtype)


def setup_inputs(seed: int = 0) -> dict:
    key = jax.random.key(seed)
    ks = jax.random.split(key, 24)
    f32 = jnp.float32
    nrm = lambda k, shape, fan_in: jax.random.normal(k, shape, f32) * (fan_in ** -0.5)
    x = jax.random.normal(ks[0], (BATCH, SEQ, D_MODEL), f32)
    ln1_g = 1.0 + 0.02 * jax.random.normal(ks[1], (DEPTH, D_MODEL), f32)
    w_in = nrm(ks[2], (DEPTH, D_MODEL, IN_TOTAL), D_MODEL)
    conv_a_w = nrm(ks[3], (DEPTH, CONV_K, CONV_WIDTH), CONV_K)
    conv_b_w = nrm(ks[4], (DEPTH, LRU_CONV_K, LRU_WIDTH), LRU_CONV_K)
    conv_b_b = 0.02 * jax.random.normal(ks[5], (DEPTH, LRU_WIDTH), f32)
    lru_wa = nrm(ks[6], (DEPTH, LRU_HEADS, LRU_HEAD_DIM, LRU_HEAD_DIM), LRU_HEAD_DIM)
    lru_ba = 0.02 * jax.random.normal(ks[7], (DEPTH, LRU_WIDTH), f32)
    lru_wx = nrm(ks[8], (DEPTH, LRU_HEADS, LRU_HEAD_DIM, LRU_HEAD_DIM), LRU_HEAD_DIM)
    lru_bx = 0.02 * jax.random.normal(ks[9], (DEPTH, LRU_WIDTH), f32)
    u = jax.random.uniform(ks[10], (DEPTH, LRU_WIDTH), f32, 0.9, 0.999)
    s = u ** (1.0 / LRU_C)
    lru_lambda = jnp.log(s) - jnp.log1p(-s)
    w_out_a = nrm(ks[11], (DEPTH, CONV_WIDTH, D_MODEL), CONV_WIDTH)
    w_out_b = nrm(ks[12], (DEPTH, LRU_WIDTH, D_MODEL), LRU_WIDTH)
    gate_bias = 0.02 * jax.random.normal(ks[13], (DEPTH, 2, D_MODEL), f32)
    w_o = nrm(ks[14], (DEPTH, D_MODEL, D_MODEL), D_MODEL)
    ln2_g = 1.0 + 0.02 * jax.random.normal(ks[15], (DEPTH, D_MODEL), f32)
    w_ffn_gate = nrm(ks[16], (DEPTH, D_MODEL, D_FF), D_MODEL)
    w_ffn_up = nrm(ks[17], (DEPTH, D_MODEL, D_FF), D_MODEL)
    w_ffn_down = nrm(ks[18], (DEPTH, D_FF, D_MODEL), D_FF)
    final_g = 1.0 + 0.02 * jax.random.normal(ks[19], (D_MODEL,), f32)
    return {"x": x, "ln1_g": ln1_g, "w_in": w_in, "conv_a_w": conv_a_w,
            "conv_b_w": conv_b_w, "conv_b_b": conv_b_b, "lru_wa": lru_wa,
            "lru_ba": lru_ba, "lru_wx": lru_wx, "lru_bx": lru_bx,
            "lru_lambda": lru_lambda, "w_out_a": w_out_a, "w_out_b": w_out_b,
            "gate_bias": gate_bias, "w_o": w_o, "ln2_g": ln2_g,
            "w_ffn_gate": w_ffn_gate, "w_ffn_up": w_ffn_up,
            "w_ffn_down": w_ffn_down, "final_g": final_g}


def reference(x, ln1_g, w_in, conv_a_w, conv_b_w, conv_b_b, lru_wa, lru_ba,
              lru_wx, lru_bx, lru_lambda, w_out_a, w_out_b, gate_bias, w_o,
              ln2_g, w_ffn_gate, w_ffn_up, w_ffn_down, final_g):
    for l in range(DEPTH):
        h = rmsnorm(x, ln1_g[l])
        proj = h @ w_in[l]
        b_a, c_a, x_a, x_b, g_b, gate_a_logit, gate_b_logit = jnp.split(proj, SPLIT_POINTS, axis=-1)
        y_a = b_a * causal_depthwise_conv(c_a * x_a, conv_a_w[l])
        u_b = causal_depthwise_conv(x_b, conv_b_w[l], conv_b_b[l])
        y_b = rg_lru(u_b, lru_wa[l], lru_ba[l], lru_wx[l], lru_bx[l], lru_lambda[l])
        y_b = y_b * jax.nn.gelu(g_b)
        merged = (jax.nn.sigmoid(gate_a_logit + gate_bias[l, 0]) * (y_a @ w_out_a[l])
                  + jax.nn.sigmoid(gate_b_logit + gate_bias[l, 1]) * (y_b @ w_out_b[l]))
        x = x + merged @ w_o[l]
        h = rmsnorm(x, ln2_g[l])
        x = x + (jax.nn.silu(h @ w_ffn_gate[l]) * (h @ w_ffn_up[l])) @ w_ffn_down[l]
    return rmsnorm(x, final_g)
```

```python
import functools

import jax
import jax.numpy as jnp
from jax import lax
from jax.experimental import pallas as pl
from jax.experimental.pallas import tpu as pltpu

SUBLANES = 8
LANES = 128
MXU_DIM = 256
VMEM_BYTES = 64 * 1024 * 1024

RMS_EPS = 1e-6
LRU_C = 8.0
CONV_A_K = 3
CONV_B_K = 4

TILE = 512
SEG = TILE // SUBLANES
FF_CHUNK = MXU_DIM


def _rmsnorm(x, g):
    var = jnp.mean(x * x, axis=-1, keepdims=True)
    return x * lax.rsqrt(var + RMS_EPS) * g


def _fill_halo(ext_ref, carry_ref, n_halo, first_tile):
    width = ext_ref.shape[-1]
    sub = lax.broadcasted_iota(jnp.int32, (SUBLANES, width), 0)

    @pl.when(first_tile)
    def _():
        carry_ref[...] = jnp.zeros_like(carry_ref)

    base = SUBLANES * n_halo
    for k in range(1, n_halo + 1):
        lo = base + TILE - SUBLANES * k
        cur = ext_ref[lo:lo + SUBLANES, :]
        slot = slice(SUBLANES * (n_halo - k), SUBLANES * (n_halo - k + 1))
        prev = carry_ref[slot, :]
        ext_ref[slot, :] = pltpu.roll(jnp.where(sub == SUBLANES - 1, prev, cur), 1, axis=0)
        carry_ref[slot, :] = cur


def _causal_conv(ext_ref, w_ref, n_taps):
    acc = None
    for k in range(n_taps):
        term = ext_ref[SUBLANES * k:SUBLANES * k + TILE, :] * w_ref[k:k + 1, :]
        acc = term if acc is None else acc + term
    return acc


def _mixer_kernel(x_ref, ln_g_ref, w_in_ref, conv_a_w_ref, conv_b_w_ref, conv_b_b_ref,
                  w_gate_ref, b_gate_ref, lam_ref, w_out_a_ref, w_out_b_ref,
                  gate_bias_ref, w_o_ref, out_ref,
                  hn_ref, ext_a_ref, ext_b_ref, carry_a_ref, carry_b_ref,
                  a_ref, b_ref, carry_h_ref):
    width = ext_a_ref.shape[-1]
    d_model = x_ref.shape[-1]
    first_tile = pl.program_id(1) == 0
    f32 = jnp.float32
    bf16 = jnp.bfloat16

    def proj(col):
        return jnp.dot(hn_ref[...], w_in_ref[:, col * width:(col + 1) * width],
                       preferred_element_type=f32)

    hn_ref[...] = _rmsnorm(x_ref[...], ln_g_ref[...]).astype(bf16)

    halo_a = SUBLANES * (CONV_A_K - 1)
    ext_a_ref[halo_a:halo_a + TILE, :] = proj(1) * proj(2)
    _fill_halo(ext_a_ref, carry_a_ref, CONV_A_K - 1, first_tile)
    y_a = (proj(0) * _causal_conv(ext_a_ref, conv_a_w_ref, CONV_A_K)).astype(bf16)
    branch_a = jnp.dot(y_a, w_out_a_ref[...], preferred_element_type=f32)
    gate_a = jax.nn.sigmoid(proj(5) + gate_bias_ref[0:1, :])
    merged = gate_a * branch_a

    halo_b = SUBLANES * (CONV_B_K - 1)
    ext_b_ref[halo_b:halo_b + TILE, :] = proj(3)
    _fill_halo(ext_b_ref, carry_b_ref, CONV_B_K - 1, first_tile)
    u = _causal_conv(ext_b_ref, conv_b_w_ref, CONV_B_K) + conv_b_b_ref[...]
    u_bf = u.astype(bf16)
    log_a_scale = -LRU_C * jax.nn.softplus(-lam_ref[...])
    n_blocks = width // MXU_DIM
    for q in range(n_blocks):
        cols = slice(q * MXU_DIM, (q + 1) * MXU_DIM)
        pre = jnp.dot(u_bf[:, cols], w_gate_ref[q], preferred_element_type=f32)
        r = jax.nn.sigmoid(pre[:, :MXU_DIM] + b_gate_ref[0:1, cols])
        i = jax.nn.sigmoid(pre[:, MXU_DIM:] + b_gate_ref[1:2, cols])
        log_a = log_a_scale[:, cols] * r
        a = jnp.exp(log_a)
        b_sq = -jnp.tanh(log_a) * (1.0 + a * a)
        a_ref[:, cols] = a
        b_ref[:, cols] = jnp.sqrt(b_sq) * (i * u[:, cols])

    h = jnp.zeros((SUBLANES, width), f32)
    p = jnp.ones((SUBLANES, width), f32)
    for j in range(SEG):
        rows = slice(SUBLANES * j, SUBLANES * (j + 1))
        a_j = a_ref[rows, :]
        h = a_j * h + b_ref[rows, :]
        p = p * a_j
        b_ref[rows, :] = h
        a_ref[rows, :] = p

    @pl.when(first_tile)
    def _():
        carry_h_ref[...] = jnp.zeros_like(carry_h_ref)

    sub = lax.broadcasted_iota(jnp.int32, (SUBLANES, width), 0)
    enter = carry_h_ref[...]
    for _ in range(SUBLANES - 1):
        end = h + p * enter
        enter = jnp.where(sub == 0, enter, pltpu.roll(end, 1, axis=0))
    carry_h_ref[...] = pltpu.roll(h + p * enter, 1, axis=0)

    enter_all = jnp.concatenate([enter] * SEG, axis=0)
    state = b_ref[...] + a_ref[...] * enter_all
    y_b = (state * jax.nn.gelu(proj(4))).astype(bf16)
    branch_b = jnp.dot(y_b, w_out_b_ref[...], preferred_element_type=f32)
    gate_b = jax.nn.sigmoid(proj(6) + gate_bias_ref[1:2, :])
    merged = (merged + gate_b * branch_b).astype(bf16)

    out_ref[...] = x_ref[...] + jnp.dot(merged, w_o_ref[...], preferred_element_type=f32)


def _ffn_kernel(x_ref, ln_g_ref, w_gate_ref, w_up_ref, w_down_ref, final_g_ref, out_ref,
                *, apply_final_norm):
    f32 = jnp.float32
    bf16 = jnp.bfloat16
    d_ff = w_gate_ref.shape[-1]
    x = x_ref[...]
    hn = _rmsnorm(x, ln_g_ref[...]).astype(bf16)
    acc = x
    for c0 in range(0, d_ff, FF_CHUNK):
        cols = slice(c0, c0 + FF_CHUNK)
        g = jnp.dot(hn, w_gate_ref[:, cols], preferred_element_type=f32)
        up = jnp.dot(hn, w_up_ref[:, cols], preferred_element_type=f32)
        act = (jax.nn.silu(g) * up).astype(bf16)
        acc = acc + jnp.dot(act, w_down_ref[cols, :], preferred_element_type=f32)
    if apply_final_norm:
        acc = _rmsnorm(acc, final_g_ref[...])
    out_ref[...] = acc


def _const_spec(shape):
    return pl.BlockSpec(shape, lambda b, t: (0,) * len(shape))


def _tile_spec(d_model):
    return pl.BlockSpec((None, TILE, d_model), lambda b, t: (b, t, 0))


def _vmem_limit(weight_bytes, act_bytes):
    return min(weight_bytes + act_bytes, VMEM_BYTES - 4 * 1024 * 1024)


def _mixer_call(x, ln_g, w_in, conv_a_w, conv_b_w, conv_b_b, w_gate, b_gate, lam,
                w_out_a, w_out_b, gate_bias, w_o):
    batch, seq, d_model = x.shape
    width = w_out_a.shape[0]
    f32 = jnp.float32
    operands = (ln_g, w_in, conv_a_w, conv_b_w, conv_b_b, w_gate, b_gate, lam,
                w_out_a, w_out_b, gate_bias, w_o)
    weight_bytes = sum(int(o.size) * o.dtype.itemsize for o in operands)
    tile_f32 = TILE * width * 4
    return pl.pallas_call(
        _mixer_kernel,
        grid=(batch, seq // TILE),
        in_specs=[_tile_spec(d_model)] + [_const_spec(o.shape) for o in operands],
        out_specs=_tile_spec(d_model),
        out_shape=jax.ShapeDtypeStruct(x.shape, f32),
        scratch_shapes=[
            pltpu.VMEM((TILE, d_model), jnp.bfloat16),
            pltpu.VMEM((TILE + SUBLANES * (CONV_A_K - 1), width), f32),
            pltpu.VMEM((TILE + SUBLANES * (CONV_B_K - 1), width), f32),
            pltpu.VMEM((SUBLANES * (CONV_A_K - 1), width), f32),
            pltpu.VMEM((SUBLANES * (CONV_B_K - 1), width), f32),
            pltpu.VMEM((TILE, width), f32),
            pltpu.VMEM((TILE, width), f32),
            pltpu.VMEM((SUBLANES, width), f32),
        ],
        compiler_params=pltpu.CompilerParams(
            dimension_semantics=("arbitrary", "arbitrary"),
            vmem_limit_bytes=_vmem_limit(weight_bytes, 16 * tile_f32)),
        name="mixer",
    )(x, *operands)


def _ffn_call(x, ln_g, w_gate, w_up, w_down, final_g, apply_final_norm):
    batch, seq, d_model = x.shape
    operands = (ln_g, w_gate, w_up, w_down, final_g)
    weight_bytes = sum(int(o.size) * o.dtype.itemsize for o in operands)
    tile_f32 = TILE * d_model * 4
    return pl.pallas_call(
        functools.partial(_ffn_kernel, apply_final_norm=apply_final_norm),
        grid=(batch, seq // TILE),
        in_specs=[_tile_spec(d_model)] + [_const_spec(o.shape) for o in operands],
        out_specs=_tile_spec(d_model),
        out_shape=jax.ShapeDtypeStruct(x.shape, jnp.float32),
        compiler_params=pltpu.CompilerParams(
            dimension_semantics=("arbitrary", "arbitrary"),
            vmem_limit_bytes=_vmem_limit(weight_bytes, 12 * tile_f32)),
        name="ffn",
    )(x, *operands)


def _block_diag_gates(w_a, w_x):
    heads, hd, _ = w_a.shape
    per = MXU_DIM // hd
    eye = jnp.eye(per, dtype=w_a.dtype)

    def blocks(w):
        w4 = w.reshape(heads // per, per, hd, hd)
        return jnp.einsum("qhde,hg->qhdge", w4, eye).reshape(heads // per, MXU_DIM, MXU_DIM)

    return jnp.concatenate([blocks(w_a), blocks(w_x)], axis=-1)


def _to_tile_order(x):
    batch, seq, d_model = x.shape
    x = x.reshape(batch, seq // TILE, SUBLANES, SEG, d_model)
    return x.transpose(0, 1, 3, 2, 4).reshape(batch, seq, d_model)


def _from_tile_order(x):
    batch, seq, d_model = x.shape
    x = x.reshape(batch, seq // TILE, SEG, SUBLANES, d_model)
    return x.transpose(0, 1, 3, 2, 4).reshape(batch, seq, d_model)


def kernel(x, ln1_g, w_in, conv_a_w, conv_b_w, conv_b_b, lru_wa, lru_ba, lru_wx, lru_bx,
           lru_lambda, w_out_a, w_out_b, gate_bias, w_o, ln2_g, w_ffn_gate, w_ffn_up,
           w_ffn_down, final_g):
    depth = w_in.shape[0]
    bf16 = jnp.bfloat16
    assert x.shape[1] % TILE == 0 and TILE % SUBLANES == 0
    row = lambda v: v.reshape(1, -1)
    h = _to_tile_order(x)
    for l in range(depth):
        h = _mixer_call(
            h, row(ln1_g[l]), w_in[l].astype(bf16), conv_a_w[l], conv_b_w[l],
            row(conv_b_b[l]), _block_diag_gates(lru_wa[l], lru_wx[l]).astype(bf16),
            jnp.stack([lru_ba[l], lru_bx[l]]), row(lru_lambda[l]),
            w_out_a[l].astype(bf16), w_out_b[l].astype(bf16), gate_bias[l],
            w_o[l].astype(bf16))
        h = _ffn_call(
            h, row(ln2_g[l]), w_ffn_gate[l].astype(bf16), w_ffn_up[l].astype(bf16),
            w_ffn_down[l].astype(bf16), row(final_g), apply_final_norm=(l == depth - 1))
    return _from_tile_order(h)
```

```python
import functools

import jax
import jax.numpy as jnp
from jax import lax
from jax.experimental import pallas as pl
from jax.experimental.pallas import tpu as pltpu

SUBLANES = 8
LANES = 128
MXU_DIM = 256
VMEM_BYTES = 64 * 1024 * 1024

RMS_EPS = 1e-6
LRU_C = 8.0
CONV_A_K = 3
CONV_B_K = 4

TILE = 512
SEG = TILE // SUBLANES
FF_CHUNK = MXU_DIM


def _rmsnorm(x, g):
    var = jnp.mean(x * x, axis=-1, keepdims=True)
    return x * lax.rsqrt(var + RMS_EPS) * g


def _fill_halo(ext_ref, carry_ref, n_halo):
    width = ext_ref.shape[-1]
    sub = lax.broadcasted_iota(jnp.int32, (SUBLANES, width), 0)
    base = SUBLANES * n_halo
    for k in range(1, n_halo + 1):
        lo = base + TILE - SUBLANES * k
        cur = ext_ref[lo:lo + SUBLANES, :]
        slot = slice(SUBLANES * (n_halo - k), SUBLANES * (n_halo - k + 1))
        prev = carry_ref[slot, :]
        ext_ref[slot, :] = pltpu.roll(jnp.where(sub == SUBLANES - 1, prev, cur), 1, axis=0)
        carry_ref[slot, :] = cur


def _causal_conv(ext_ref, w_ref, n_taps):
    acc = None
    for k in range(n_taps):
        term = ext_ref[SUBLANES * k:SUBLANES * k + TILE, :] * w_ref[k:k + 1, :]
        acc = term if acc is None else acc + term
    return acc


def _mixer_kernel(x_ref, ln_g_ref, w_in_ref, conv_a_w_ref, conv_b_w_ref, conv_b_b_ref,
                  w_gate_ref, b_gate_ref, lam_ref, w_out_a_ref, w_out_b_ref,
                  gate_bias_ref, w_o_ref, out_ref,
                  hn_ref, ext_a_ref, ext_b_ref, carry_a_ref, carry_b_ref,
                  a_ref, b_ref, carry_h_ref):
    width = ext_a_ref.shape[-1]
    d_model = x_ref.shape[-1]
    f32 = jnp.float32
    bf16 = jnp.bfloat16

    @pl.when(pl.program_id(1) == 0)
    def _():
        carry_a_ref[...] = jnp.zeros_like(carry_a_ref)
        carry_b_ref[...] = jnp.zeros_like(carry_b_ref)
        carry_h_ref[...] = jnp.zeros_like(carry_h_ref)

    def proj(col):
        return jnp.dot(hn_ref[...], w_in_ref[:, col * width:(col + 1) * width],
                       preferred_element_type=f32)

    hn_ref[...] = _rmsnorm(x_ref[...], ln_g_ref[...]).astype(bf16)

    halo_a = SUBLANES * (CONV_A_K - 1)
    ext_a_ref[halo_a:halo_a + TILE, :] = proj(1) * proj(2)
    _fill_halo(ext_a_ref, carry_a_ref, CONV_A_K - 1)
    y_a = (proj(0) * _causal_conv(ext_a_ref, conv_a_w_ref, CONV_A_K)).astype(bf16)
    branch_a = jnp.dot(y_a, w_out_a_ref[...], preferred_element_type=f32)
    gate_a = jax.nn.sigmoid(proj(5) + gate_bias_ref[0:1, :])
    merged = gate_a * branch_a

    halo_b = SUBLANES * (CONV_B_K - 1)
    ext_b_ref[halo_b:halo_b + TILE, :] = proj(3)
    _fill_halo(ext_b_ref, carry_b_ref, CONV_B_K - 1)
    u = _causal_conv(ext_b_ref, conv_b_w_ref, CONV_B_K) + conv_b_b_ref[...]
    u_bf = u.astype(bf16)
    log_a_scale = -LRU_C * jax.nn.softplus(-lam_ref[...])
    n_blocks = width // MXU_DIM
    for q in range(n_blocks):
        cols = slice(q * MXU_DIM, (q + 1) * MXU_DIM)
        pre = jnp.dot(u_bf[:, cols], w_gate_ref[q], preferred_element_type=f32)
        r = jax.nn.sigmoid(pre[:, :MXU_DIM] + b_gate_ref[0:1, cols])
        i = jax.nn.sigmoid(pre[:, MXU_DIM:] + b_gate_ref[1:2, cols])
        log_a = log_a_scale[:, cols] * r
        a = jnp.exp(log_a)
        b_sq = -jnp.tanh(log_a) * (1.0 + a * a)
        a_ref[:, cols] = a
        b_ref[:, cols] = jnp.sqrt(b_sq) * (i * u[:, cols])

    h = jnp.zeros((SUBLANES, width), f32)
    p = jnp.ones((SUBLANES, width), f32)
    for j in range(SEG):
        rows = slice(SUBLANES * j, SUBLANES * (j + 1))
        a_j = a_ref[rows, :]
        h = a_j * h + b_ref[rows, :]
        p = p * a_j
        b_ref[rows, :] = h
        a_ref[rows, :] = p

    sub = lax.broadcasted_iota(jnp.int32, (SUBLANES, width), 0)
    enter = carry_h_ref[...]
    for _ in range(SUBLANES - 1):
        end = h + p * enter
        enter = jnp.where(sub == 0, enter, pltpu.roll(end, 1, axis=0))
    carry_h_ref[...] = pltpu.roll(h + p * enter, 1, axis=0)

    enter_all = jnp.concatenate([enter] * SEG, axis=0)
    state = b_ref[...] + a_ref[...] * enter_all
    y_b = (state * jax.nn.gelu(proj(4))).astype(bf16)
    branch_b = jnp.dot(y_b, w_out_b_ref[...], preferred_element_type=f32)
    gate_b = jax.nn.sigmoid(proj(6) + gate_bias_ref[1:2, :])
    merged = (merged + gate_b * branch_b).astype(bf16)

    out_ref[...] = x_ref[...] + jnp.dot(merged, w_o_ref[...], preferred_element_type=f32)


def _ffn_kernel(x_ref, ln_g_ref, w_gate_ref, w_up_ref, w_down_ref, final_g_ref, out_ref,
                *, apply_final_norm):
    f32 = jnp.float32
    bf16 = jnp.bfloat16
    d_ff = w_gate_ref.shape[-1]
    x = x_ref[...]
    hn = _rmsnorm(x, ln_g_ref[...]).astype(bf16)
    acc = x
    for c0 in range(0, d_ff, FF_CHUNK):
        cols = slice(c0, c0 + FF_CHUNK)
        g = jnp.dot(hn, w_gate_ref[:, cols], preferred_element_type=f32)
        up = jnp.dot(hn, w_up_ref[:, cols], preferred_element_type=f32)
        act = (jax.nn.silu(g) * up).astype(bf16)
        acc = acc + jnp.dot(act, w_down_ref[cols, :], preferred_element_type=f32)
    if apply_final_norm:
        acc = _rmsnorm(acc, final_g_ref[...])
    out_ref[...] = acc


def _layer_spec(stacked, layer):
    zeros = (0,) * (stacked.ndim - 1)
    return pl.BlockSpec((None,) + stacked.shape[1:], lambda b, t: (layer,) + zeros,
                        pipeline_mode=pl.Buffered(1))


def _tile_spec(d_model):
    return pl.BlockSpec((None, TILE, d_model), lambda b, t: (b, t, 0))


def _layer_bytes(stacked):
    return sum(int(o.size) // o.shape[0] * o.dtype.itemsize for o in stacked)


def _vmem_limit(weight_bytes, act_bytes):
    return min(weight_bytes + act_bytes, VMEM_BYTES - 4 * 1024 * 1024)


def _mixer_call(x, layer, stacked):
    batch, seq, d_model = x.shape
    width = stacked[-1].shape[-2]
    f32 = jnp.float32
    tile_f32 = TILE * width * 4
    return pl.pallas_call(
        _mixer_kernel,
        grid=(batch, seq // TILE),
        in_specs=[_tile_spec(d_model)] + [_layer_spec(o, layer) for o in stacked],
        out_specs=_tile_spec(d_model),
        out_shape=jax.ShapeDtypeStruct(x.shape, f32),
        scratch_shapes=[
            pltpu.VMEM((TILE, d_model), jnp.bfloat16),
            pltpu.VMEM((TILE + SUBLANES * (CONV_A_K - 1), width), f32),
            pltpu.VMEM((TILE + SUBLANES * (CONV_B_K - 1), width), f32),
            pltpu.VMEM((SUBLANES * (CONV_A_K - 1), width), f32),
            pltpu.VMEM((SUBLANES * (CONV_B_K - 1), width), f32),
            pltpu.VMEM((TILE, width), f32),
            pltpu.VMEM((TILE, width), f32),
            pltpu.VMEM((SUBLANES, width), f32),
        ],
        compiler_params=pltpu.CompilerParams(
            dimension_semantics=("arbitrary", "arbitrary"),
            vmem_limit_bytes=_vmem_limit(_layer_bytes(stacked), 16 * tile_f32)),
        name="mixer",
    )(x, *stacked)


def _ffn_call(x, layer, stacked, final_g, apply_final_norm):
    batch, seq, d_model = x.shape
    tile_f32 = TILE * d_model * 4
    return pl.pallas_call(
        functools.partial(_ffn_kernel, apply_final_norm=apply_final_norm),
        grid=(batch, seq // TILE),
        in_specs=([_tile_spec(d_model)] + [_layer_spec(o, layer) for o in stacked]
                  + [_layer_spec(final_g, 0)]),
        out_specs=_tile_spec(d_model),
        out_shape=jax.ShapeDtypeStruct(x.shape, jnp.float32),
        compiler_params=pltpu.CompilerParams(
            dimension_semantics=("arbitrary", "arbitrary"),
            vmem_limit_bytes=_vmem_limit(_layer_bytes(stacked), 12 * tile_f32)),
        name="ffn",
    )(x, *stacked, final_g)


def _block_diag_gates(w_a, w_x):
    depth, heads, hd, _ = w_a.shape
    per = MXU_DIM // hd
    eye = jnp.eye(per, dtype=w_a.dtype)

    def blocks(w):
        w5 = w.reshape(depth, heads // per, per, hd, hd)
        return jnp.einsum("lqhde,hg->lqhdge", w5, eye).reshape(
            depth, heads // per, MXU_DIM, MXU_DIM)

    return jnp.concatenate([blocks(w_a), blocks(w_x)], axis=-1)


def _to_tile_order(x):
    batch, seq, d_model = x.shape
    x = x.reshape(batch, seq // TILE, SUBLANES, SEG, d_model)
    return x.transpose(0, 1, 3, 2, 4).reshape(batch, seq, d_model)


def _from_tile_order(x):
    batch, seq, d_model = x.shape
    x = x.reshape(batch, seq // TILE, SEG, SUBLANES, d_model)
    return x.transpose(0, 1, 3, 2, 4).reshape(batch, seq, d_model)


def kernel(x, ln1_g, w_in, conv_a_w, conv_b_w, conv_b_b, lru_wa, lru_ba, lru_wx, lru_bx,
           lru_lambda, w_out_a, w_out_b, gate_bias, w_o, ln2_g, w_ffn_gate, w_ffn_up,
           w_ffn_down, final_g):
    depth = w_in.shape[0]
    bf16 = jnp.bfloat16
    assert x.shape[1] % TILE == 0 and TILE % SUBLANES == 0
    rows = lambda v: v[:, None, :]
    mixer_params = (
        rows(ln1_g), w_in.astype(bf16), conv_a_w, conv_b_w, rows(conv_b_b),
        _block_diag_gates(lru_wa, lru_wx).astype(bf16), jnp.stack([lru_ba, lru_bx], axis=1),
        rows(lru_lambda), w_out_a.astype(bf16), w_out_b.astype(bf16), gate_bias,
        w_o.astype(bf16))
    ffn_params = (rows(ln2_g), w_ffn_gate.astype(bf16), w_ffn_up.astype(bf16),
                  w_ffn_down.astype(bf16))
    final_row = final_g.reshape(1, 1, -1)
    h = _to_tile_order(x)
    for l in range(depth):
        h = _mixer_call(h, l, mixer_params)
        h = _ffn_call(h, l, ffn_params, final_row, apply_final_norm=(l == depth - 1))
    return _from_tile_order(h)
```
